```python
import math
import jax
import jax.numpy as jnp
from jax import lax
import numpy as np

D_MODEL = 2048
BATCH = 2
SEQ = 4096
DEPTH = 1
DEC_BATCH = 32
DEC_SEQ = 8
PAST_LEN = 16384
PAGE_SIZE = 128

HEAD_DIM = 128
N_HEADS_MOBA = D_MODEL // (2 * HEAD_DIM)
N_HEADS_FOX = D_MODEL // (2 * HEAD_DIM)
W_MOBA = N_HEADS_MOBA * HEAD_DIM
W_FOX = N_HEADS_FOX * HEAD_DIM
MOBA_BLOCK = 256
MOBA_TOPK = 3
MOBA_QBLOCK = 64
FOX_QBLOCK = 128
N_BUCKETS = 32
MAX_DISTANCE = 128
D_FF = 256 * ((8 * D_MODEL + 3 * 256 - 1) // (3 * 256))
FFN_SCALE = 0.5
RMS_EPS = 1e-6
ATTN_SCALE = HEAD_DIM ** -0.5
IN_SIZES = (W_MOBA, W_MOBA, W_MOBA, W_FOX, W_FOX, W_FOX, N_HEADS_FOX, D_MODEL, D_MODEL)
N_IN = sum(IN_SIZES)

kernel_name = 'moba_fox_macaron_gated_decoder_step'


def rmsnorm(x, g):
    xf = x.astype(jnp.float32)
    y = xf * lax.rsqrt(jnp.mean(xf * xf, axis=-1, keepdims=True) + RMS_EPS)
    return (y * g.astype(jnp.float32)).astype(x.dtype)


def swiglu(x, w_gate, w_up, w_down):
    return (jax.nn.silu(x @ w_gate) * (x @ w_up)) @ w_down


def t5_bucket(dist):
    n = jnp.maximum(dist, 0)
    max_exact = N_BUCKETS // 2
    nf = jnp.maximum(n, max_exact).astype(jnp.float32)
    log_b = max_exact + (jnp.log(nf / max_exact) / math.log(MAX_DISTANCE / max_exact)
                         * (N_BUCKETS - max_exact)).astype(jnp.int32)
    return jnp.where(n < max_exact, n, jnp.minimum(log_b, N_BUCKETS - 1))


def gather_pages(pool, page_table):
    rows = pool[page_table]
    return rows.reshape((page_table.shape[0], page_table.shape[1] * pool.shape[1]) + pool.shape[2:])


def moba_attend(q, qpos, kb, vb, kmean, rel_bias):
    B, H, NB = kb.shape[0], kb.shape[1], kb.shape[2]
    Q = q.shape[1]
    own = qpos // MOBA_BLOCK
    fully_past = jnp.arange(NB)[None, :] < own[:, None]
    gate = jnp.einsum('bqhd,bhnd->bqhn', q.astype(jnp.float32), kmean)
    gate = jnp.where(fully_past[None, :, None, :], gate, -jnp.inf)
    top_val, top_idx = lax.top_k(gate, min(MOBA_TOPK, NB))
    own_idx = jnp.broadcast_to(own[None, :, None, None], (B, Q, H, 1)).astype(top_idx.dtype)
    sel = jnp.concatenate([top_idx, own_idx], axis=-1)
    sel_ok = jnp.concatenate([jnp.isfinite(top_val), jnp.ones((B, Q, H, 1), bool)], axis=-1)
    bi = jnp.arange(B)[:, None, None, None]
    hi = jnp.arange(H)[None, None, :, None]
    kg = kb[bi, hi, sel]
    vg = vb[bi, hi, sel]
    kpos = sel[..., None] * MOBA_BLOCK + jnp.arange(MOBA_BLOCK)
    dist = qpos[None, :, None, None, None] - kpos
    ok = sel_ok[..., None] & (dist >= 0)
    bias = rel_bias[t5_bucket(dist), hi[..., None]].astype(jnp.float32)
    logits = jnp.einsum('bqhd,bqhjsd->bqhjs', q, kg, preferred_element_type=jnp.float32) * ATTN_SCALE + bias
    logits = jnp.where(ok, logits, -jnp.inf)
    p = jax.nn.softmax(logits.reshape(B, Q, H, -1), axis=-1).reshape(logits.shape)
    return jnp.einsum('bqhjs,bqhjsd->bqhd', p.astype(vg.dtype), vg)


def moba_mixer(q, k, v, q_start, q_block, rel_bias):
    B, L, H, _ = k.shape
    Q = q.shape[1]
    nb = -(-L // MOBA_BLOCK)
    pad = nb * MOBA_BLOCK - L

    def blocks(t):
        t = jnp.pad(t, ((0, 0), (0, pad), (0, 0), (0, 0)))
        return t.reshape(B, nb, MOBA_BLOCK, H, HEAD_DIM).transpose(0, 3, 1, 2, 4)

    kb, vb = blocks(k), blocks(v)
    kmean = jnp.mean(kb.astype(jnp.float32), axis=3)
    n_chunks = Q // q_block
    qc = q.reshape(B, n_chunks, q_block, H, HEAD_DIM).swapaxes(0, 1)

    def one(args):
        q_i, c = args
        qpos = q_start + c * q_block + jnp.arange(q_block)
        return moba_attend(q_i, qpos, kb, vb, kmean, rel_bias)

    out = lax.map(one, (qc, jnp.arange(n_chunks)))
    return out.swapaxes(0, 1).reshape(B, Q, H * HEAD_DIM)


def fox_mixer(q, k, v, logf, q_start, q_block):
    B, L, H, _ = k.shape
    Q = q.shape[1]
    n_chunks = Q // q_block
    cum = jnp.cumsum(logf.astype(jnp.float32), axis=1)
    cum_k = cum.transpose(0, 2, 1)
    cum_q = cum[:, q_start:q_start + Q].reshape(B, n_chunks, q_block, H).transpose(1, 0, 3, 2)
    qc = q.reshape(B, n_chunks, q_block, H, HEAD_DIM).swapaxes(0, 1)
    kpos = jnp.arange(L)

    def one(args):
        q_i, cq_i, c = args
        qpos = q_start + c * q_block + jnp.arange(q_block)
        logits = jnp.einsum('bqhd,bshd->bhqs', q_i, k, preferred_element_type=jnp.float32) * ATTN_SCALE
        logits = logits + cq_i[..., None] - cum_k[:, :, None, :]
        logits = jnp.where(kpos[None, :] <= qpos[:, None], logits, -jnp.inf)
        p = jax.nn.softmax(logits, axis=-1)
        return jnp.einsum('bhqs,bshd->bqhd', p.astype(v.dtype), v)

    out = lax.map(one, (qc, cum_q, jnp.arange(n_chunks)))
    return out.swapaxes(0, 1).reshape(B, Q, H * HEAD_DIM)


def token_mixer(h, past, q_start, moba_qblock, fox_qblock, rel_bias, w_in, b_forget,
                q_norm_moba, k_norm_moba, q_norm_fox, k_norm_fox, w_o_moba, w_o_fox, w_out):
    B, S, _ = h.shape
    proj = h @ w_in
    q_a, k_a, v_a, q_b, k_b, v_b, f_b, g_a, g_b = jnp.split(proj, np.cumsum(IN_SIZES)[:-1].tolist(), axis=-1)

    def heads(t, n):
        return t.reshape(B, S, n, HEAD_DIM)

    q_a = rmsnorm(heads(q_a, N_HEADS_MOBA), q_norm_moba)
    k_a = rmsnorm(heads(k_a, N_HEADS_MOBA), k_norm_moba)
    v_a = heads(v_a, N_HEADS_MOBA)
    q_b = rmsnorm(heads(q_b, N_HEADS_FOX), q_norm_fox)
    k_b = rmsnorm(heads(k_b, N_HEADS_FOX), k_norm_fox)
    v_b = heads(v_b, N_HEADS_FOX)
    logf = jax.nn.log_sigmoid(f_b.astype(jnp.float32) + b_forget.astype(jnp.float32))
    new_rows = (k_a, v_a, k_b, v_b, logf)
    if past is None:
        ka_all, va_all, kb_all, vb_all, lf_all = new_rows
    else:
        ka_all, va_all, kb_all, vb_all, lf_all = (jnp.concatenate([p_, n_], axis=1) for p_, n_ in zip(past, new_rows))
    o_a = moba_mixer(q_a, ka_all, va_all, q_start, moba_qblock, rel_bias)
    o_b = fox_mixer(q_b, kb_all, vb_all, lf_all, q_start, fox_qblock)
    merged = jax.nn.sigmoid(g_a) * (o_a @ w_o_moba) + jax.nn.sigmoid(g_b) * (o_b @ w_o_fox)
    return merged @ w_out, new_rows


def decoder_layer(x, past, q_start, moba_qblock, fox_qblock, rel_bias,
                  ffn1_norm, ffn1_w_gate, ffn1_w_up, ffn1_w_down, mix_norm, w_in, b_forget,
                  q_norm_moba, k_norm_moba, q_norm_fox, k_norm_fox, w_o_moba, w_o_fox, w_out,
                  ffn2_norm, ffn2_w_gate, ffn2_w_up, ffn2_w_down):
    x = x + FFN_SCALE * swiglu(rmsnorm(x, ffn1_norm), ffn1_w_gate, ffn1_w_up, ffn1_w_down)
    y, new_rows = token_mixer(rmsnorm(x, mix_norm), past, q_start, moba_qblock, fox_qblock, rel_bias,
                              w_in, b_forget, q_norm_moba, k_norm_moba, q_norm_fox, k_norm_fox,
                              w_o_moba, w_o_fox, w_out)
    x = x + y
    x = x + FFN_SCALE * swiglu(rmsnorm(x, ffn2_norm), ffn2_w_gate, ffn2_w_up, ffn2_w_down)
    return x, new_rows


def setup_inputs(seed: int = 0) -> dict:
    key = jax.random.key(seed)
    ks = jax.random.split(key, 32)
    n_pages = PAST_LEN // PAGE_SIZE
    n_used = DEC_BATCH * n_pages
    n_pool = n_used + max(1, n_used // 4)
    f32 = jnp.float32

    def nrm(k, shape, s):
        return jax.random.normal(k, shape, f32) * s

    def gain(k, shape):
        return 1.0 + 0.02 * jax.random.normal(k, shape, f32)

    page_table = jax.random.permutation(ks[0], n_pool)[:n_used].reshape(DEC_BATCH, n_pages).astype(jnp.int32)
    logf_pool = jax.nn.log_sigmoid(jax.random.uniform(ks[6], (DEPTH, n_pool, PAGE_SIZE, N_HEADS_FOX), f32, 1.0, 4.0)
                                   + jax.random.normal(ks[7], (DEPTH, n_pool, PAGE_SIZE, N_HEADS_FOX), f32))
    return {
        'x_prompt': nrm(ks[1], (BATCH, SEQ, D_MODEL), 1.0),
        'x_sample': nrm(ks[2], (DEC_BATCH, DEC_SEQ, D_MODEL), 1.0),
        'cache_k_moba': nrm(ks[3], (DEPTH, n_pool, PAGE_SIZE, N_HEADS_MOBA, HEAD_DIM), 1.0),
        'cache_v_moba': nrm(ks[4], (DEPTH, n_pool, PAGE_SIZE, N_HEADS_MOBA, HEAD_DIM), 1.0),
        'cache_k_fox': nrm(ks[5], (DEPTH, n_pool, PAGE_SIZE, N_HEADS_FOX, HEAD_DIM), 1.0),
        'cache_v_fox': nrm(ks[8], (DEPTH, n_pool, PAGE_SIZE, N_HEADS_FOX, HEAD_DIM), 1.0),
        'cache_logf_fox': logf_pool,
        'page_table': page_table,
        'rel_bias': nrm(ks[9], (N_BUCKETS, N_HEADS_MOBA), 0.5),
        'ffn1_norm': gain(ks[10], (DEPTH, D_MODEL)),
        'ffn1_w_gate': nrm(ks[11], (DEPTH, D_MODEL, D_FF), D_MODEL ** -0.5),
        'ffn1_w_up': nrm(ks[12], (DEPTH, D_MODEL, D_FF), D_MODEL ** -0.5),
        'ffn1_w_down': nrm(ks[13], (DEPTH, D_FF, D_MODEL), D_FF ** -0.5),
        'mix_norm': gain(ks[14], (DEPTH, D_MODEL)),
        'w_in': nrm(ks[15], (DEPTH, D_MODEL, N_IN), D_MODEL ** -0.5),
        'b_forget': jax.random.uniform(ks[16], (DEPTH, N_HEADS_FOX), f32, 1.0, 4.0),
        'q_norm_moba': gain(ks[17], (DEPTH, HEAD_DIM)),
        'k_norm_moba': gain(ks[18], (DEPTH, HEAD_DIM)),
        'q_norm_fox': gain(ks[19], (DEPTH, HEAD_DIM)),
        'k_norm_fox': gain(ks[20], (DEPTH, HEAD_DIM)),
        'w_o_moba': nrm(ks[21], (DEPTH, W_MOBA, D_MODEL), W_MOBA ** -0.5),
        'w_o_fox': nrm(ks[22], (DEPTH, W_FOX, D_MODEL), W_FOX ** -0.5),
        'w_out': nrm(ks[23], (DEPTH, D_MODEL, D_MODEL), D_MODEL ** -0.5),
        'ffn2_norm': gain(ks[24], (DEPTH, D_MODEL)),
        'ffn2_w_gate': nrm(ks[25], (DEPTH, D_MODEL, D_FF), D_MODEL ** -0.5),
        'ffn2_w_up': nrm(ks[26], (DEPTH, D_MODEL, D_FF), D_MODEL ** -0.5),
        'ffn2_w_down': nrm(ks[27], (DEPTH, D_FF, D_MODEL), D_FF ** -0.5),
    }


def reference(x_prompt, x_sample, cache_k_moba, cache_v_moba, cache_k_fox, cache_v_fox, cache_logf_fox,
              page_table, rel_bias, ffn1_norm, ffn1_w_gate, ffn1_w_up, ffn1_w_down, mix_norm, w_in, b_forget,
              q_norm_moba, k_norm_moba, q_norm_fox, k_norm_fox, w_o_moba, w_o_fox, w_out,
              ffn2_norm, ffn2_w_gate, ffn2_w_up, ffn2_w_down):
    layer_params = (ffn1_norm, ffn1_w_gate, ffn1_w_up, ffn1_w_down, mix_norm, w_in, b_forget,
                    q_norm_moba, k_norm_moba, q_norm_fox, k_norm_fox, w_o_moba, w_o_fox, w_out,
                    ffn2_norm, ffn2_w_gate, ffn2_w_up, ffn2_w_down)
    caches = (cache_k_moba, cache_v_moba, cache_k_fox, cache_v_fox, cache_logf_fox)
    seq_p = x_prompt.shape[1]
    seq_s = x_sample.shape[1]
    past_len = page_table.shape[1] * PAGE_SIZE
    hp, hs = x_prompt, x_sample
    rows_p, rows_s = [], []
    for l in range(DEPTH):
        w_l = tuple(w[l] for w in layer_params)
        hp, new_p = decoder_layer(hp, None, 0, min(MOBA_QBLOCK, seq_p), min(FOX_QBLOCK, seq_p), rel_bias, *w_l)
        past = tuple(gather_pages(c[l], page_table) for c in caches)
        hs, new_s = decoder_layer(hs, past, past_len, seq_s, seq_s, rel_bias, *w_l)
        rows_p.append(new_p)
        rows_s.append(new_s)

    def stack(rows, i):
        return jnp.stack([r[i] for r in rows])

    new_k_moba_prompt = stack(rows_p, 0)
    new_v_moba_prompt = stack(rows_p, 1)
    new_k_fox_prompt = stack(rows_p, 2)
    new_v_fox_prompt = stack(rows_p, 3)
    new_logf_fox_prompt = stack(rows_p, 4)
    new_k_moba_sample = stack(rows_s, 0)
    new_v_moba_sample = stack(rows_s, 1)
    new_k_fox_sample = stack(rows_s, 2)
    new_v_fox_sample = stack(rows_s, 3)
    new_logf_fox_sample = stack(rows_s, 4)
    return (hp, hs, new_k_moba_prompt, new_v_moba_prompt, new_k_fox_prompt, new_v_fox_prompt, new_logf_fox_prompt,
            new_k_moba_sample, new_v_moba_sample, new_k_fox_sample, new_v_fox_sample, new_logf_fox_sample)
```

```python
import math

import numpy as np
import jax
import jax.numpy as jnp
from jax import lax
from jax.experimental import pallas as pl
from jax.experimental.pallas import tpu as pltpu

F32 = jnp.float32
BF16 = jnp.bfloat16

HEAD_DIM = 128
N_HEADS = 8
W_MIX = N_HEADS * HEAD_DIM
PAGE = 128
MOBA_BLOCK = 256
MOBA_TOPK = 3
N_BUCKETS = 32
MAX_DISTANCE = 128
FFN_SCALE = 0.5
RMS_EPS = 1e-6
ATTN_SCALE = HEAD_DIM ** -0.5
NEG_INF = float("-inf")

VMEM_LIMIT = 56 * 1024 * 1024
TM_FFN = 768
TF_FFN = 512
TM_OUT = 256
TQ = 256
CH = 4


def _params(sem, **kw):
    return pltpu.CompilerParams(dimension_semantics=sem, vmem_limit_bytes=VMEM_LIMIT, **kw)


def _nt_dot(a, b, **kw):
    return lax.dot_general(a, b, (((1,), (1,)), ((), ())), preferred_element_type=F32, **kw)


def _split3(x):
    hi = x.astype(BF16)
    r = x - hi.astype(F32)
    mid = r.astype(BF16)
    lo = (r - mid.astype(F32)).astype(BF16)
    return hi, mid, lo


def _dot3_left(x, w):
    return sum(jnp.dot(p, w, preferred_element_type=F32) for p in _split3(x))


def _dot3_right(u, x):
    return sum(jnp.dot(u, p, preferred_element_type=F32) for p in _split3(x))


def _rms(x, g):
    return x * lax.rsqrt(jnp.mean(x * x, axis=-1, keepdims=True) + RMS_EPS) * g


def _ffn_kernel(x_ref, g_ref, wg_ref, wu_ref, wd_ref, o_ref, xn_ref, acc_ref):
    f = pl.program_id(1)

    @pl.when(f == 0)
    def _():
        xn_ref[...] = _rms(x_ref[...], g_ref[...]).astype(BF16)
        acc_ref[...] = jnp.zeros_like(acc_ref)

    xn = xn_ref[...]
    gate = jnp.dot(xn, wg_ref[...], preferred_element_type=F32)
    up = jnp.dot(xn, wu_ref[...], preferred_element_type=F32)
    act = (gate * jax.nn.sigmoid(gate) * up).astype(BF16)
    acc_ref[...] += jnp.dot(act, wd_ref[...], preferred_element_type=F32)

    @pl.when(f == pl.num_programs(1) - 1)
    def _():
        o_ref[...] = x_ref[...] + FFN_SCALE * acc_ref[...]


def _ffn(x, g, wg, wu, wd):
    t, d = x.shape
    dff = wg.shape[1]
    return pl.pallas_call(
        _ffn_kernel,
        grid=(t // TM_FFN, dff // TF_FFN),
        in_specs=[
            pl.BlockSpec((TM_FFN, d), lambda i, f: (i, 0)),
            pl.BlockSpec((1, d), lambda i, f: (0, 0)),
            pl.BlockSpec((d, TF_FFN), lambda i, f: (0, f)),
            pl.BlockSpec((d, TF_FFN), lambda i, f: (0, f)),
            pl.BlockSpec((TF_FFN, d), lambda i, f: (f, 0)),
        ],
        out_specs=pl.BlockSpec((TM_FFN, d), lambda i, f: (i, 0)),
        out_shape=jax.ShapeDtypeStruct((t, d), F32),
        scratch_shapes=[pltpu.VMEM((TM_FFN, d), BF16), pltpu.VMEM((TM_FFN, d), F32)],
        compiler_params=_params(("parallel", "arbitrary")),
        name="ffn",
    )(x, g.reshape(1, d), wg, wu, wd)


N_GROUPS = 10


def _proj_kernel(x_ref, g_ref, w_ref, wf_ref, bf_ref, gains_ref, p_ref, lf_ref, h_ref):
    j = pl.program_id(1)

    @pl.when(j == 0)
    def _():
        h = _rms(x_ref[...], g_ref[...]).astype(BF16)
        h_ref[...] = h
        z = jnp.dot(h, wf_ref[...], preferred_element_type=F32) + bf_ref[...]
        lf_ref[...] = jnp.minimum(z, 0.0) - jnp.log1p(jnp.exp(-jnp.abs(z)))

    y = jnp.dot(h_ref[...], w_ref[...], preferred_element_type=F32)
    is_qk = (j == 0) | (j == 1) | (j == 3) | (j == 4)
    is_v = (j == 2) | (j == 5)

    @pl.when(is_qk)
    def _():
        for hh in range(N_HEADS):
            sl = slice(hh * HEAD_DIM, (hh + 1) * HEAD_DIM)
            p_ref[0, :, sl] = _rms(y[:, sl], gains_ref[0, :, sl])

    @pl.when(is_v)
    def _():
        p_ref[0] = y

    @pl.when(j >= 6)
    def _():
        p_ref[0] = jax.nn.sigmoid(y)


def _proj(x, g, w_main, w_f, b_f, gains):
    t, d = x.shape
    return pl.pallas_call(
        _proj_kernel,
        grid=(t // TM_FFN, N_GROUPS),
        in_specs=[
            pl.BlockSpec((TM_FFN, d), lambda i, j: (i, 0)),
            pl.BlockSpec((1, d), lambda i, j: (0, 0)),
            pl.BlockSpec((d, W_MIX), lambda i, j: (0, j)),
            pl.BlockSpec((d, 128), lambda i, j: (0, 0)),
            pl.BlockSpec((1, 128), lambda i, j: (0, 0)),
            pl.BlockSpec((1, 1, W_MIX), lambda i, j: (j, 0, 0)),
        ],
        out_specs=[
            pl.BlockSpec((1, TM_FFN, W_MIX), lambda i, j: (j, i, 0)),
            pl.BlockSpec((TM_FFN, 128), lambda i, j: (i, 0)),
        ],
        out_shape=[jax.ShapeDtypeStruct((N_GROUPS, t, W_MIX), F32),
                   jax.ShapeDtypeStruct((t, 128), F32)],
        scratch_shapes=[pltpu.VMEM((TM_FFN, d), BF16)],
        compiler_params=_params(("parallel", "arbitrary")),
        name="proj",
    )(x, g.reshape(1, d), w_main, w_f, b_f, gains)


def _suffix_consts():
    r = np.arange(PAGE * N_HEADS)
    p_in, h_in = r // N_HEADS, r % N_HEADS
    h_out, p_out = r // PAGE, r % PAGE
    same = h_in[:, None] == h_out[None, :]
    within = same & (p_in[:, None] > p_out[None, :])
    pages = np.arange(PAGE)
    later = pages[None, :] > pages[:, None]
    return (jnp.asarray(within, BF16), jnp.asarray(same, BF16), jnp.asarray(later, BF16))


def _suffix_compute(x, wm_ref, bm_ref, u_ref):
    within = _dot3_left(x, wm_ref[...])
    later = _dot3_right(u_ref[...], x)
    return within + _dot3_left(later, bm_ref[...])


def _suffix_kernel(x_ref, wm_ref, bm_ref, u_ref, o_ref):
    o_ref[0] = _suffix_compute(x_ref[0], wm_ref, bm_ref, u_ref)


def _suffix(x):
    nb = x.shape[0]
    wm, bm, u = _suffix_consts()
    cw = PAGE * N_HEADS
    return pl.pallas_call(
        _suffix_kernel,
        grid=(nb,),
        in_specs=[
            pl.BlockSpec((1, PAGE, cw), lambda b: (b, 0, 0)),
            pl.BlockSpec((cw, cw), lambda b: (0, 0)),
            pl.BlockSpec((cw, cw), lambda b: (0, 0)),
            pl.BlockSpec((PAGE, PAGE), lambda b: (0, 0)),
        ],
        out_specs=pl.BlockSpec((1, PAGE, cw), lambda b: (b, 0, 0)),
        out_shape=jax.ShapeDtypeStruct((nb, PAGE, cw), F32),
        compiler_params=_params(("arbitrary",)),
        name="suffix_prompt",
    )(x, wm, bm, u)


def _suffix_paged_kernel(pt_ref, pool_ref, wm_ref, bm_ref, u_ref, o_ref, x_sc, sem):
    b = pl.program_id(0)
    n_pages = x_sc.shape[0]

    def row_copy(i):
        page = pt_ref[b * n_pages + i]
        return pltpu.make_async_copy(pool_ref.at[pl.ds(page, 1)], x_sc.at[pl.ds(i, 1)], sem.at[0])

    def start(i, c):
        row_copy(i).start()
        return c

    def wait(i, c):
        row_copy(i).wait()
        return c

    lax.fori_loop(0, n_pages, start, 0)
    lax.fori_loop(0, n_pages, wait, 0)
    o_ref[0] = _suffix_compute(x_sc[...], wm_ref, bm_ref, u_ref)


def _suffix_paged(pool2d, pt_flat, nb, n_pages):
    wm, bm, u = _suffix_consts()
    cw = PAGE * N_HEADS
    grid_spec = pltpu.PrefetchScalarGridSpec(
        num_scalar_prefetch=1,
        grid=(nb,),
        in_specs=[
            pl.BlockSpec(memory_space=pl.ANY),
            pl.BlockSpec((cw, cw), lambda b, pt: (0, 0)),
            pl.BlockSpec((cw, cw), lambda b, pt: (0, 0)),
            pl.BlockSpec((PAGE, PAGE), lambda b, pt: (0, 0)),
        ],
        out_specs=pl.BlockSpec((1, n_pages, cw), lambda b, pt: (b, 0, 0)),
        scratch_shapes=[pltpu.VMEM((n_pages, cw), F32), pltpu.SemaphoreType.DMA((1,))],
    )
    return pl.pallas_call(
        _suffix_paged_kernel,
        grid_spec=grid_spec,
        out_shape=jax.ShapeDtypeStruct((nb, n_pages, cw), F32),
        compiler_params=_params(("arbitrary",)),
        name="suffix_paged",
    )(pt_flat, pool2d, wm, bm, u)


def _top3_mask(gate, lane):
    sel = jnp.zeros(gate.shape, F32)
    g = gate
    for _ in range(MOBA_TOPK):
        mx = jnp.max(g, axis=1, keepdims=True)
        first = jnp.min(jnp.where(g == mx, lane, 128), axis=1, keepdims=True)
        hit = lane == first
        ok = (mx > NEG_INF) & (mx < float("inf"))
        sel = jnp.where(hit & ok, 1.0, sel)
        g = jnp.where(hit, NEG_INF, g)
    return sel


def _online_update(s, v_bf, m, l, acc):
    m_new = jnp.maximum(m, jnp.max(s, axis=1, keepdims=True))
    alpha = jnp.exp(m - m_new)
    p = jnp.exp(s - m_new)
    l_new = alpha * l + jnp.sum(p, axis=1, keepdims=True)
    acc_new = alpha * acc + jnp.dot(p.astype(BF16), v_bf, preferred_element_type=F32)
    return m_new, l_new, acc_new


def _moba_prompt_kernel(q_ref, k_ref, v_ref, bt_ref, b31_ref, o_ref, kbf, vbf, kmean):
    qi = pl.program_id(2)
    nblk = k_ref.shape[1] // TQ

    @pl.when(qi == 0)
    def _():
        kbf[...] = k_ref[0].astype(BF16)
        vbf[...] = v_ref[0].astype(BF16)
        kmean[...] = jnp.zeros_like(kmean)
        for n in range(nblk):
            kmean[n:n + 1, :] = jnp.mean(k_ref[0, n * TQ:(n + 1) * TQ, :], axis=0, keepdims=True)

    q = q_ref[0]
    lane = lax.broadcasted_iota(jnp.int32, (TQ, 128), 1)
    gate = _nt_dot(q, kmean[...], precision=lax.Precision.HIGHEST)
    gate = jnp.where(lane < qi, gate, NEG_INF)
    sel = _top3_mask(gate, lane)

    qs = (q * ATTN_SCALE).astype(BF16)
    row = lax.broadcasted_iota(jnp.int32, (TQ, TQ), 0)
    col = lax.broadcasted_iota(jnp.int32, (TQ, TQ), 1)

    off = pl.multiple_of(qi * TQ, TQ)
    s = _nt_dot(qs, kbf[pl.ds(off, TQ), :]) + bt_ref[0, 0]
    s = jnp.where(col <= row, s, NEG_INF)
    m0 = jnp.full((TQ, 1), NEG_INF, F32)
    state = _online_update(s, vbf[pl.ds(off, TQ), :], m0, jnp.zeros((TQ, 1), F32), jnp.zeros((TQ, HEAD_DIM), F32))

    b31 = b31_ref[0][:, :1]

    def body(j, st):
        o = pl.multiple_of(j * TQ, TQ)
        picked = jnp.sum(jnp.where(lane == j, sel, 0.0), axis=1, keepdims=True) > 0.0
        bias = jnp.where(j == qi - 1, bt_ref[0, 1], b31)
        sj = _nt_dot(qs, kbf[pl.ds(o, TQ), :]) + bias
        sj = jnp.where(picked, sj, NEG_INF)
        return _online_update(sj, vbf[pl.ds(o, TQ), :], *st)

    m, l, acc = lax.fori_loop(0, qi, body, state)
    o_ref[...] = acc / l


def _moba_prompt(p, bias_tiles, b31, n_batch, seq):
    nq = seq // TQ
    return pl.pallas_call(
        _moba_prompt_kernel,
        grid=(n_batch, N_HEADS, nq),
        in_specs=[
            pl.BlockSpec((1, TQ, HEAD_DIM), lambda b, h, i: (0, b * nq + i, h)),
            pl.BlockSpec((1, seq, HEAD_DIM), lambda b, h, i: (1, b, h)),
            pl.BlockSpec((1, seq, HEAD_DIM), lambda b, h, i: (2, b, h)),
            pl.BlockSpec((1, 2, TQ, TQ), lambda b, h, i: (h, 0, 0, 0)),
            pl.BlockSpec((1, 1, 128), lambda b, h, i: (h, 0, 0)),
        ],
        out_specs=pl.BlockSpec((TQ, HEAD_DIM), lambda b, h, i: (b * nq + i, h)),
        out_shape=jax.ShapeDtypeStruct((n_batch * seq, W_MIX), F32),
        scratch_shapes=[pltpu.VMEM((seq, HEAD_DIM), BF16), pltpu.VMEM((seq, HEAD_DIM), BF16),
                        pltpu.VMEM((128, HEAD_DIM), F32)],
        compiler_params=_params(("parallel", "parallel", "arbitrary")),
        name="moba_prompt",
    )(p, p, p, bias_tiles, b31)


def _fox_prompt_kernel(q_ref, k_ref, v_ref, suf_ref, o_ref, kbf, vbf):
    qi = pl.program_id(2)

    @pl.when(qi == 0)
    def _():
        kbf[...] = k_ref[0].astype(BF16)
        vbf[...] = v_ref[0].astype(BF16)

    qs = (q_ref[0] * ATTN_SCALE).astype(BF16)
    state = (jnp.full((TQ, 1), NEG_INF, F32), jnp.zeros((TQ, 1), F32), jnp.zeros((TQ, HEAD_DIM), F32))

    def body(j, st):
        o = pl.multiple_of(j * TQ, TQ)
        sj = _nt_dot(qs, kbf[pl.ds(o, TQ), :]) + suf_ref[0, 0, j]
        return _online_update(sj, vbf[pl.ds(o, TQ), :], *st)

    state = lax.fori_loop(0, qi, body, state)
    row = lax.broadcasted_iota(jnp.int32, (TQ, TQ), 0)
    col = lax.broadcasted_iota(jnp.int32, (TQ, TQ), 1)
    off = pl.multiple_of(qi * TQ, TQ)
    s = _nt_dot(qs, kbf[pl.ds(off, TQ), :]) + suf_ref[0, 0, qi]
    s = jnp.where(col <= row, s, NEG_INF)
    m, l, acc = _online_update(s, vbf[pl.ds(off, TQ), :], *state)
    o_ref[...] = acc / l


def _fox_prompt(p, suf, n_batch, seq):
    nq = seq // TQ
    return pl.pallas_call(
        _fox_prompt_kernel,
        grid=(n_batch, N_HEADS, nq),
        in_specs=[
            pl.BlockSpec((1, TQ, HEAD_DIM), lambda b, h, i: (3, b * nq + i, h)),
            pl.BlockSpec((1, seq, HEAD_DIM), lambda b, h, i: (4, b, h)),
            pl.BlockSpec((1, seq, HEAD_DIM), lambda b, h, i: (5, b, h)),
            pl.BlockSpec((1, 1, nq, 1, TQ), lambda b, h, i: (b, h, 0, 0, 0)),
        ],
        out_specs=pl.BlockSpec((TQ, HEAD_DIM), lambda b, h, i: (b * nq + i, h)),
        out_shape=jax.ShapeDtypeStruct((n_batch * seq, W_MIX), F32),
        scratch_shapes=[pltpu.VMEM((seq, HEAD_DIM), BF16), pltpu.VMEM((seq, HEAD_DIM), BF16)],
        compiler_params=_params(("parallel", "parallel", "arbitrary")),
        name="fox_prompt",
    )(p, p, p, suf)


CHUNK_ROWS = CH * PAGE * N_HEADS
CHUNK_KEYS = CH * PAGE


def _page_stream(pt_ref, kpool, vpool, kbuf, vbuf, sem, n_pages):
    rows = PAGE * N_HEADS

    def copies(bb, c, slot):
        out = []
        for i in range(CH):
            page = pt_ref[bb * n_pages + c * CH + i]
            dst = pl.ds(i * rows, rows)
            out.append(pltpu.make_async_copy(kpool.at[page], kbuf.at[slot, dst], sem.at[0, slot]))
            out.append(pltpu.make_async_copy(vpool.at[page], vbuf.at[slot, dst], sem.at[1, slot]))
        return out

    def start(bb, c, slot):
        for cp in copies(bb, c, slot):
            cp.start()

    def wait(bb, c, slot):
        for cp in copies(bb, c, slot):
            cp.wait()

    return start, wait


def _stream_chunks(b, n_batch, n_chunks, start, wait, compute):
    @pl.when(b == 0)
    def _():
        start(0, 0, 0)

    def body(c, carry):
        slot = lax.rem(c, 2)
        wait(b, c, slot)

        @pl.when(c + 1 < n_chunks)
        def _():
            start(b, c + 1, 1 - slot)

        @pl.when((c + 1 == n_chunks) & (b + 1 < n_batch))
        def _():
            start(b + 1, 0, 1 - slot)

        compute(c, slot)
        return carry

    lax.fori_loop(0, n_chunks, body, 0)


def _load_tail(b, kn_ref, vn_ref, ktail, vtail):
    n_new = kn_ref.shape[0]

    @pl.when(b == 0)
    def _():
        ktail[...] = jnp.zeros_like(ktail)
        vtail[...] = jnp.zeros_like(vtail)

    for h in range(N_HEADS):
        sl = slice(h * HEAD_DIM, (h + 1) * HEAD_DIM)
        ktail[h, 0:n_new, :] = kn_ref[:, sl].astype(BF16)
        vtail[h, 0:n_new, :] = vn_ref[:, sl].astype(BF16)


def _fox_decode_kernel(pt_ref, q_ref, kn_ref, vn_ref, lfn_ref, suf_ref, tri_ref, kpool, vpool, o_ref,
                       kbuf, vbuf, sem, ktail, vtail, m_sc, l_sc, acc_sc):
    b = pl.program_id(0)
    n_batch = pl.num_programs(0)
    n_chunks = suf_ref.shape[2]
    n_new = q_ref.shape[0]
    start, wait = _page_stream(pt_ref, kpool, vpool, kbuf, vbuf, sem, n_chunks * CH)

    m_sc[...] = jnp.full_like(m_sc, NEG_INF)
    l_sc[...] = jnp.zeros_like(l_sc)
    acc_sc[...] = jnp.zeros_like(acc_sc)
    _load_tail(b, kn_ref, vn_ref, ktail, vtail)

    def head_q(h):
        return (q_ref[:, h * HEAD_DIM:(h + 1) * HEAD_DIM] * ATTN_SCALE).astype(BF16)

    def update(h, s, v_bf):
        m, l, acc = _online_update(s, v_bf, m_sc[h][:, :1], l_sc[h][:, :1], acc_sc[h])
        m_sc[h] = jnp.broadcast_to(m, (n_new, 128))
        l_sc[h] = jnp.broadcast_to(l, (n_new, 128))
        acc_sc[h] = acc

    def compute(c, slot):
        for h in range(N_HEADS):
            k_bf = kbuf[slot, pl.ds(h, CHUNK_KEYS, stride=N_HEADS), :].astype(BF16)
            v_bf = vbuf[slot, pl.ds(h, CHUNK_KEYS, stride=N_HEADS), :].astype(BF16)
            s = _nt_dot(head_q(h), k_bf) + suf_ref[0, h, c]
            update(h, s, v_bf)

    _stream_chunks(b, n_batch, n_chunks, start, wait, compute)

    prefix = _dot3_left(lfn_ref[0], tri_ref[...])
    row = lax.broadcasted_iota(jnp.int32, (n_new, 128), 0)
    col = lax.broadcasted_iota(jnp.int32, (n_new, 128), 1)
    for h in range(N_HEADS):
        s = _nt_dot(head_q(h), ktail[h]) - prefix[h:h + 1, :]
        s = jnp.where((col <= row) & (col < n_new), s, NEG_INF)
        update(h, s, vtail[h])
        o_ref[:, h * HEAD_DIM:(h + 1) * HEAD_DIM] = acc_sc[h] / l_sc[h][:, :1]


def _moba_decode_kernel(pt_ref, q_ref, kn_ref, vn_ref, bpast_ref, bown_ref, kpool, vpool, o_ref,
                        kbuf, vbuf, sem, ktail, vtail, kmean, m_sc, l_sc, acc_sc):
    b = pl.program_id(0)
    n_batch = pl.num_programs(0)
    n_blocks = acc_sc.shape[0]
    blocks_per_chunk = CHUNK_KEYS // MOBA_BLOCK
    n_chunks = n_blocks // blocks_per_chunk
    n_new = q_ref.shape[0]
    start, wait = _page_stream(pt_ref, kpool, vpool, kbuf, vbuf, sem, n_chunks * CH)

    _load_tail(b, kn_ref, vn_ref, ktail, vtail)

    @pl.when(b == 0)
    def _():
        kmean[...] = jnp.zeros_like(kmean)
        m_sc[...] = jnp.zeros_like(m_sc)
        l_sc[...] = jnp.zeros_like(l_sc)

    lane = lax.broadcasted_iota(jnp.int32, (n_new, 128), 1)
    blk_rows = MOBA_BLOCK * N_HEADS

    def head_q(h):
        return (q_ref[:, h * HEAD_DIM:(h + 1) * HEAD_DIM] * ATTN_SCALE).astype(BF16)

    def compute(c, slot):
        for j in range(blocks_per_chunk):
            n = c * blocks_per_chunk + j
            kk = kbuf[slot, j * blk_rows:(j + 1) * blk_rows, :].reshape(MOBA_BLOCK, N_HEADS, HEAD_DIM)
            kmean[pl.ds(pl.multiple_of(n * N_HEADS, N_HEADS), N_HEADS), :] = jnp.sum(kk, axis=0) * (1.0 / MOBA_BLOCK)
            last = jnp.where(n == n_blocks - 1, 1, 0)
            for h in range(N_HEADS):
                k_bf = kbuf[slot, pl.ds(j * blk_rows + h, MOBA_BLOCK, stride=N_HEADS), :].astype(BF16)
                v_bf = vbuf[slot, pl.ds(j * blk_rows + h, MOBA_BLOCK, stride=N_HEADS), :].astype(BF16)
                s = _nt_dot(head_q(h), k_bf) + bpast_ref[last, h]
                m = jnp.max(s, axis=1, keepdims=True)
                p = jnp.exp(s - m)
                m_sc[h] = jnp.where(lane == n, m, m_sc[h])
                l_sc[h] = jnp.where(lane == n, jnp.sum(p, axis=1, keepdims=True), l_sc[h])
                acc_sc[n, h] = jnp.dot(p.astype(BF16), v_bf, preferred_element_type=F32)

    _stream_chunks(b, n_batch, n_chunks, start, wait, compute)

    row = lax.broadcasted_iota(jnp.int32, (n_new, 128), 0)
    for h in range(N_HEADS):
        q_h = q_ref[:, h * HEAD_DIM:(h + 1) * HEAD_DIM]
        km_h = kmean[pl.ds(h, 128, stride=N_HEADS), :]
        gate = _nt_dot(q_h, km_h, precision=lax.Precision.HIGHEST)
        gate = jnp.where(lane < n_blocks, gate, NEG_INF)
        sel = _top3_mask(gate, lane) > 0.0

        s_own = _nt_dot(head_q(h), ktail[h]) + bown_ref[h]
        s_own = jnp.where((lane <= row) & (lane < n_new), s_own, NEG_INF)
        m_own = jnp.max(s_own, axis=1, keepdims=True)
        p_own = jnp.exp(s_own - m_own)
        l_own = jnp.sum(p_own, axis=1, keepdims=True)
        acc_own = jnp.dot(p_own.astype(BF16), vtail[h], preferred_element_type=F32)

        m_blk = m_sc[h]
        m_all = jnp.maximum(jnp.max(jnp.where(sel, m_blk, NEG_INF), axis=1, keepdims=True), m_own)
        w = jnp.where(sel, jnp.exp(m_blk - m_all), 0.0)
        w_own = jnp.exp(m_own - m_all)
        den = jnp.sum(w * l_sc[h], axis=1, keepdims=True) + w_own * l_own

        def merge(n, num):
            wn = jnp.sum(jnp.where(lane == n, w, 0.0), axis=1, keepdims=True)
            return num + wn * acc_sc[n, h]

        num = lax.fori_loop(0, n_blocks, merge, w_own * acc_own)
        o_ref[:, h * HEAD_DIM:(h + 1) * HEAD_DIM] = num / den


def _decode_scratch(n_new):
    return [
        pltpu.VMEM((2, CHUNK_ROWS, HEAD_DIM), F32),
        pltpu.VMEM((2, CHUNK_ROWS, HEAD_DIM), F32),
        pltpu.SemaphoreType.DMA((2, 2)),
        pltpu.VMEM((N_HEADS, 128, HEAD_DIM), BF16),
        pltpu.VMEM((N_HEADS, 128, HEAD_DIM), BF16),
    ]


def _fox_decode(p_s, lfn_t, suf, tri, kpool, vpool, pt_flat, n_batch, n_new, row0):
    n_chunks = suf.shape[2]
    rows = lambda grp: pl.BlockSpec((1, n_new, W_MIX), lambda b, pt, g=grp: (g, row0 + b, 0))
    grid_spec = pltpu.PrefetchScalarGridSpec(
        num_scalar_prefetch=1,
        grid=(n_batch,),
        in_specs=[
            rows(3), rows(4), rows(5),
            pl.BlockSpec((1, N_HEADS, 128), lambda b, pt: (b, 0, 0)),
            pl.BlockSpec((1, N_HEADS, n_chunks, 1, CHUNK_KEYS), lambda b, pt: (b, 0, 0, 0, 0)),
            pl.BlockSpec((128, 128), lambda b, pt: (0, 0)),
            pl.BlockSpec(memory_space=pl.ANY),
            pl.BlockSpec(memory_space=pl.ANY),
        ],
        out_specs=pl.BlockSpec((n_new, W_MIX), lambda b, pt: (b, 0)),
        scratch_shapes=_decode_scratch(n_new) + [
            pltpu.VMEM((N_HEADS, n_new, 128), F32),
            pltpu.VMEM((N_HEADS, n_new, 128), F32),
            pltpu.VMEM((N_HEADS, n_new, HEAD_DIM), F32),
        ],
    )

    def kern(pt_ref, q_ref, kn_ref, vn_ref, lfn_ref, suf_ref, tri_ref, kp, vp, o_ref, *scratch):
        _fox_decode_kernel(pt_ref, q_ref.at[0], kn_ref.at[0], vn_ref.at[0], lfn_ref, suf_ref, tri_ref, kp, vp,
                           o_ref, *scratch)

    return pl.pallas_call(
        kern,
        grid_spec=grid_spec,
        out_shape=jax.ShapeDtypeStruct((n_batch * n_new, W_MIX), F32),
        compiler_params=_params(("arbitrary",)),
        name="fox_decode",
    )(pt_flat, p_s, p_s, p_s, lfn_t, suf, tri, kpool, vpool)


def _moba_decode(p_s, bpast, bown, kpool, vpool, pt_flat, n_batch, n_new, row0, n_blocks):
    rows = lambda grp: pl.BlockSpec((1, n_new, W_MIX), lambda b, pt, g=grp: (g, row0 + b, 0))
    grid_spec = pltpu.PrefetchScalarGridSpec(
        num_scalar_prefetch=1,
        grid=(n_batch,),
        in_specs=[
            rows(0), rows(1), rows(2),
            pl.BlockSpec((2, N_HEADS, n_new, MOBA_BLOCK), lambda b, pt: (0, 0, 0, 0)),
            pl.BlockSpec((N_HEADS, n_new, 128), lambda b, pt: (0, 0, 0)),
            pl.BlockSpec(memory_space=pl.ANY),
            pl.BlockSpec(memory_space=pl.ANY),
        ],
        out_specs=pl.BlockSpec((n_new, W_MIX), lambda b, pt: (b, 0)),
        scratch_shapes=_decode_scratch(n_new) + [
            pltpu.VMEM((128 * N_HEADS, HEAD_DIM), F32),
            pltpu.VMEM((N_HEADS, n_new, 128), F32),
            pltpu.VMEM((N_HEADS, n_new, 128), F32),
            pltpu.VMEM((n_blocks, N_HEADS, n_new, HEAD_DIM), F32),
        ],
    )

    def kern(pt_ref, q_ref, kn_ref, vn_ref, bpast_ref, bown_ref, kp, vp, o_ref, *scratch):
        _moba_decode_kernel(pt_ref, q_ref.at[0], kn_ref.at[0], vn_ref.at[0], bpast_ref, bown_ref, kp, vp,
                            o_ref, *scratch)

    return pl.pallas_call(
        kern,
        grid_spec=grid_spec,
        out_shape=jax.ShapeDtypeStruct((n_batch * n_new, W_MIX), F32),
        compiler_params=_params(("arbitrary",)),
        name="moba_decode",
    )(pt_flat, p_s, p_s, p_s, bpast, bown, kpool, vpool)


def _out_kernel(x_ref, oa_ref, ob_ref, ga_ref, gb_ref, wa_ref, wb_ref, wo_ref, o_ref):
    ya = jnp.dot(oa_ref[...].astype(BF16), wa_ref[...], preferred_element_type=F32)
    yb = jnp.dot(ob_ref[...].astype(BF16), wb_ref[...], preferred_element_type=F32)
    ga = jnp.concatenate([ga_ref[0], ga_ref[1]], axis=1)
    gb = jnp.concatenate([gb_ref[0], gb_ref[1]], axis=1)
    merged = (ga * ya + gb * yb).astype(BF16)
    o_ref[...] = x_ref[...] + jnp.dot(merged, wo_ref[...], preferred_element_type=F32)


def _out_proj(x, oa, ob, p, wa, wb, wo):
    t, d = x.shape
    const = lambda shape: pl.BlockSpec(shape, lambda i: (0, 0), pipeline_mode=pl.Buffered(1))
    return pl.pallas_call(
        _out_kernel,
        grid=(t // TM_OUT,),
        in_specs=[
            pl.BlockSpec((TM_OUT, d), lambda i: (i, 0)),
            pl.BlockSpec((TM_OUT, W_MIX), lambda i: (i, 0)),
            pl.BlockSpec((TM_OUT, W_MIX), lambda i: (i, 0)),
            pl.BlockSpec((2, TM_OUT, W_MIX), lambda i: (3, i, 0)),
            pl.BlockSpec((2, TM_OUT, W_MIX), lambda i: (4, i, 0)),
            const((W_MIX, d)), const((W_MIX, d)), const((d, d)),
        ],
        out_specs=pl.BlockSpec((TM_OUT, d), lambda i: (i, 0)),
        out_shape=jax.ShapeDtypeStruct((t, d), F32),
        compiler_params=_params(("parallel",)),
        name="out_proj",
    )(x, oa, ob, p, p, wa, wb, wo)


def _t5_bucket_np(dist):
    n = np.maximum(dist, 0)
    max_exact = N_BUCKETS // 2
    nf = np.maximum(n, max_exact).astype(np.float64)
    log_b = max_exact + (np.log(nf / max_exact) / math.log(MAX_DISTANCE / max_exact)
                         * (N_BUCKETS - max_exact)).astype(np.int64)
    return np.where(n < max_exact, n, np.minimum(log_b, N_BUCKETS - 1))


def _bias_tables(rel_bias, n_new):
    rb = rel_bias.astype(F32).T
    r = np.arange(TQ)
    d0 = r[:, None] - r[None, :]
    prompt_idx = np.stack([_t5_bucket_np(d0), _t5_bucket_np(d0 + MOBA_BLOCK)])
    prompt_tiles = rb[:, prompt_idx]
    far = jnp.broadcast_to(rb[:, N_BUCKETS - 1][:, None, None], (N_HEADS, 1, 128))
    q = np.arange(n_new)
    s = np.arange(MOBA_BLOCK)
    last_idx = _t5_bucket_np(MOBA_BLOCK + q[:, None] - s[None, :])
    far_idx = np.full_like(last_idx, N_BUCKETS - 1)
    dec_past = jnp.transpose(rb[:, np.stack([far_idx, last_idx])], (1, 0, 2, 3))
    i = np.arange(128)
    dec_own = rb[:, _t5_bucket_np(q[:, None] - i[None, :])]
    return prompt_tiles, far, dec_past, dec_own


def kernel(x_prompt, x_sample, cache_k_moba, cache_v_moba, cache_k_fox, cache_v_fox, cache_logf_fox, page_table,
           rel_bias, ffn1_norm, ffn1_w_gate, ffn1_w_up, ffn1_w_down, mix_norm, w_in, b_forget, q_norm_moba,
           k_norm_moba, q_norm_fox, k_norm_fox, w_o_moba, w_o_fox, w_out, ffn2_norm, ffn2_w_gate, ffn2_w_up,
           ffn2_w_down):
    depth = ffn1_norm.shape[0]
    assert depth == 1, "single-layer step"
    n_batch, seq, d = x_prompt.shape
    dec_batch, n_new, _ = x_sample.shape
    n_pages = page_table.shape[1]
    n_pool = cache_k_moba.shape[1]
    t_p = n_batch * seq
    t_s = dec_batch * n_new
    past_len = n_pages * PAGE
    assert seq % TQ == 0 and t_p % n_new == 0 and n_pages % CH == 0 and past_len % MOBA_BLOCK == 0
    assert n_pages == PAGE and seq // PAGE <= PAGE

    bf = lambda a: a.astype(BF16)
    x = jnp.concatenate([x_prompt.reshape(t_p, d), x_sample.reshape(t_s, d)], axis=0)

    x1 = _ffn(x, ffn1_norm[0], bf(ffn1_w_gate[0]), bf(ffn1_w_up[0]), bf(ffn1_w_down[0]))

    wi = w_in[0]
    n_qkv = 6 * W_MIX
    w_main = bf(jnp.concatenate([wi[:, :n_qkv], wi[:, n_qkv + N_HEADS:]], axis=1))
    w_f = bf(jnp.pad(wi[:, n_qkv:n_qkv + N_HEADS], ((0, 0), (0, 128 - N_HEADS))))
    b_f = jnp.pad(b_forget[0].astype(F32), (0, 128 - N_HEADS)).reshape(1, 128)
    ones = jnp.ones((W_MIX,), F32)
    tile = lambda g: jnp.tile(g.astype(F32), N_HEADS)
    gains = jnp.stack([tile(q_norm_moba[0]), tile(k_norm_moba[0]), ones, tile(q_norm_fox[0]), tile(k_norm_fox[0]),
                       ones, ones, ones, ones, ones]).reshape(N_GROUPS, 1, W_MIX)
    p, lf = _proj(x1, mix_norm[0], w_main, w_f, b_f, gains)
    logf = lf[:, :N_HEADS]

    prompt_tiles, far, dec_past, dec_own = _bias_tables(rel_bias, n_new)

    nq = seq // TQ
    lf_p = logf[:t_p].reshape(n_batch, seq // PAGE, PAGE * N_HEADS)
    lf_p = jnp.pad(lf_p, ((0, 0), (0, PAGE - seq // PAGE), (0, 0)))
    suf_p = _suffix(lf_p)[:, :seq // PAGE].reshape(n_batch, seq // PAGE, N_HEADS, PAGE)
    suf_p = jnp.transpose(suf_p, (0, 2, 1, 3)).reshape(n_batch, N_HEADS, nq, 1, TQ)
    o_a_p = _moba_prompt(p, prompt_tiles, far, n_batch, seq)
    o_b_p = _fox_prompt(p, suf_p, n_batch, seq)

    pt_flat = page_table.reshape(-1).astype(jnp.int32)
    rows = PAGE * N_HEADS
    pool3 = lambda c: c[0].reshape(n_pool, rows, HEAD_DIM)
    suf_s = _suffix_paged(cache_logf_fox[0].reshape(n_pool, rows), pt_flat, dec_batch, n_pages)
    suf_s = jnp.transpose(suf_s.reshape(dec_batch, n_pages, N_HEADS, PAGE), (0, 2, 1, 3))
    suf_s = suf_s.reshape(dec_batch, N_HEADS, n_pages // CH, 1, CHUNK_KEYS)
    lfn_t = jnp.transpose(logf[t_p:].reshape(dec_batch, n_new, N_HEADS), (0, 2, 1))
    lfn_t = jnp.pad(lfn_t, ((0, 0), (0, 0), (0, 128 - n_new)))
    tri = jnp.asarray(np.arange(128)[:, None] <= np.arange(128)[None, :], BF16)
    row0 = t_p // n_new
    o_b_s = _fox_decode(p, lfn_t, suf_s, tri, pool3(cache_k_fox), pool3(cache_v_fox), pt_flat, dec_batch, n_new, row0)
    o_a_s = _moba_decode(p, dec_past, dec_own, pool3(cache_k_moba), pool3(cache_v_moba), pt_flat, dec_batch, n_new,
                         row0, past_len // MOBA_BLOCK)

    o_a = jnp.concatenate([o_a_p, o_a_s], axis=0)
    o_b = jnp.concatenate([o_b_p, o_b_s], axis=0)
    x2 = _out_proj(x1, o_a, o_b, p, bf(w_o_moba[0]), bf(w_o_fox[0]), bf(w_out[0]))

    x3 = _ffn(x2, ffn2_norm[0], bf(ffn2_w_gate[0]), bf(ffn2_w_up[0]), bf(ffn2_w_down[0]))

    def rows_of(grp, lo, hi, lead):
        return p[grp, lo:hi].reshape((1,) + lead + (N_HEADS, HEAD_DIM))

    y_prompt = x3[:t_p].reshape(n_batch, seq, d)
    y_sample = x3[t_p:].reshape(dec_batch, n_new, d)
    lead_p, lead_s = (n_batch, seq), (dec_batch, n_new)
    return (y_prompt, y_sample,
            rows_of(1, 0, t_p, lead_p), rows_of(2, 0, t_p, lead_p),
            rows_of(4, 0, t_p, lead_p), rows_of(5, 0, t_p, lead_p),
            logf[:t_p].reshape((1,) + lead_p + (N_HEADS,)),
            rows_of(1, t_p, t_p + t_s, lead_s), rows_of(2, t_p, t_p + t_s, lead_s),
            rows_of(4, t_p, t_p + t_s, lead_s), rows_of(5, t_p, t_p + t_s, lead_s),
            logf[t_p:].reshape((1,) + lead_s + (N_HEADS,)))
```
